```python
import jax, jax.numpy as jnp
from jax import lax
import numpy as np

D_MODEL = 2048
BATCH = 2
SEQ = 8192
DEPTH = 1

CHUNK = 64
EPS = 1e-5
N_BRANCHES = 2
SGU_WIDTH = D_MODEL
SGU_BLOCK = 128
SGU_GROUPS = 16
SGU_GROUP_DIM = SGU_WIDTH // SGU_GROUPS
SSD_WIDTH = D_MODEL
SSD_HEADDIM = 64
SSD_HEADS = SSD_WIDTH // SSD_HEADDIM
SSD_GROUPS = 4
SSD_HEADS_PER_GROUP = SSD_HEADS // SSD_GROUPS
SSD_STATE = 128
CONV_WIDTH = 4
XBC_WIDTH = SSD_WIDTH + 2 * SSD_GROUPS * SSD_STATE
IN_SPLITS = (SGU_WIDTH, SGU_WIDTH, SGU_WIDTH, SSD_WIDTH, XBC_WIDTH, SSD_HEADS, N_BRANCHES * D_MODEL)
IN_PROJ_WIDTH = sum(IN_SPLITS)
IN_OFFSETS = tuple(int(v) for v in np.cumsum(IN_SPLITS)[:-1])

kernel_name = "hybrid_sgu_ssd_gated_block"


def rmsnorm(x, w):
    xf = x.astype(jnp.float32)
    xf = xf * lax.rsqrt(jnp.mean(xf * xf, axis=-1, keepdims=True) + EPS)
    return (xf * w.astype(jnp.float32)).astype(x.dtype)


def layernorm(x, g, b):
    xf = x.astype(jnp.float32)
    mu = jnp.mean(xf, axis=-1, keepdims=True)
    var = jnp.mean(jnp.square(xf - mu), axis=-1, keepdims=True)
    y = (xf - mu) * lax.rsqrt(var + EPS) * g.astype(jnp.float32) + b.astype(jnp.float32)
    return y.astype(x.dtype)


def gated_group_rmsnorm(y, z, w):
    h = (y * jax.nn.silu(z)).astype(jnp.float32)
    shp = h.shape
    h = h.reshape(shp[:-1] + (SSD_GROUPS, shp[-1] // SSD_GROUPS))
    h = h * lax.rsqrt(jnp.mean(h * h, axis=-1, keepdims=True) + EPS)
    return (h.reshape(shp) * w.astype(jnp.float32)).astype(y.dtype)


def causal_depthwise_conv(x, w, b):
    S = x.shape[1]
    xp = jnp.pad(x, ((0, 0), (CONV_WIDTH - 1, 0), (0, 0)))
    out = b
    for k in range(CONV_WIDTH):
        out = out + xp[:, k:k + S, :] * w[k]
    return out


def sgu_mixer(u, v, z, norm_g, norm_b, w_s, b_s):
    Bsz, S, _ = u.shape
    nb = S // SGU_BLOCK
    vn = layernorm(v, norm_g, norm_b).reshape(Bsz, nb, SGU_BLOCK, SGU_GROUPS, SGU_GROUP_DIM)
    pos_chunk = jnp.arange(SGU_BLOCK) // CHUNK
    mask = pos_chunk[None, :] <= pos_chunk[:, None]
    w = jnp.where(mask[None], w_s, jnp.zeros_like(w_s))
    mixed = jnp.einsum("gij,bnjgc->bnigc", w, vn) + b_s.T[None, None, :, :, None]
    mixed = mixed.reshape(Bsz, S, SGU_WIDTH)
    return u * mixed * jax.nn.silu(z)


def ssd_mixer(xbc, z, dt_raw, conv_w, conv_b, dt_bias, A_log, D_skip, norm_w):
    Bsz, S, _ = xbc.shape
    nc = S // CHUNK
    G, R, P, N, L = SSD_GROUPS, SSD_HEADS_PER_GROUP, SSD_HEADDIM, SSD_STATE, CHUNK
    xbc = jax.nn.silu(causal_depthwise_conv(xbc, conv_w, conv_b))
    xs, Bm, Cm = jnp.split(xbc, (SSD_WIDTH, SSD_WIDTH + G * N), axis=-1)
    x = xs.reshape(Bsz, nc, L, G, R, P)
    Bm = Bm.reshape(Bsz, nc, L, G, N)
    Cm = Cm.reshape(Bsz, nc, L, G, N)
    dt = jax.nn.softplus(dt_raw.astype(jnp.float32) + dt_bias.astype(jnp.float32))
    dt = dt.reshape(Bsz, nc, L, G, R)
    A = -jnp.exp(A_log.astype(jnp.float32)).reshape(G, R)
    a = jnp.transpose(dt * A, (0, 1, 3, 4, 2))
    acs = jnp.cumsum(a, axis=-1)
    idx = jnp.arange(L)
    causal = idx[:, None] >= idx[None, :]
    seg = acs[..., :, None] - acs[..., None, :]
    Lmat = jnp.exp(jnp.where(causal, seg, -jnp.inf))
    x_dt = x.astype(jnp.float32) * dt[..., None]
    cb = jnp.einsum("bclgn,bcsgn->bcgls", Cm, Bm)
    y_diag = jnp.einsum("bcgls,bcgrls,bcsgrp->bclgrp", cb, Lmat, x_dt)
    decay_states = jnp.exp(acs[..., -1:] - acs)
    states = jnp.einsum("bclgn,bcgrl,bclgrp->bcgrpn", Bm, decay_states, x_dt)
    chunk_decay = jnp.exp(acs[..., -1])

    def step(h, inp):
        dec, st = inp
        return dec[..., None, None] * h + st, h

    h0 = jnp.zeros((Bsz, G, R, P, N), dtype=states.dtype)
    _, prev = lax.scan(step, h0, (jnp.moveaxis(chunk_decay, 1, 0), jnp.moveaxis(states, 1, 0)))
    prev = jnp.moveaxis(prev, 0, 1)
    y_off = jnp.einsum("bclgn,bcgrl,bcgrpn->bclgrp", Cm, jnp.exp(acs), prev)
    y = y_diag + y_off + x.astype(jnp.float32) * D_skip.astype(jnp.float32).reshape(G, R)[..., None]
    y = y.reshape(Bsz, S, SSD_WIDTH).astype(xbc.dtype)
    return gated_group_rmsnorm(y, z, norm_w)


def setup_inputs(seed: int = 0) -> dict:
    key = jax.random.key(seed)
    ks = jax.random.split(key, 20)
    f32 = jnp.float32
    nrm = lambda k, shp, s: jax.random.normal(k, shp, f32) * s
    dt_init = jnp.exp(jax.random.uniform(ks[9], (DEPTH, SSD_HEADS), f32, np.log(1e-3), np.log(1e-1)))
    return {
        "x": jax.random.normal(ks[0], (BATCH, SEQ, D_MODEL), f32),
        "norm_w": 1.0 + nrm(ks[1], (DEPTH, D_MODEL), 0.02),
        "w_in": nrm(ks[2], (DEPTH, D_MODEL, IN_PROJ_WIDTH), D_MODEL ** -0.5),
        "gate_b": nrm(ks[3], (DEPTH, N_BRANCHES * D_MODEL), 0.1),
        "sgu_norm_g": 1.0 + nrm(ks[4], (DEPTH, SGU_WIDTH), 0.02),
        "sgu_norm_b": nrm(ks[5], (DEPTH, SGU_WIDTH), 0.02),
        "sgu_w": nrm(ks[6], (DEPTH, SGU_GROUPS, SGU_BLOCK, SGU_BLOCK), SGU_BLOCK ** -0.5),
        "sgu_b": 1.0 + nrm(ks[7], (DEPTH, SGU_GROUPS, SGU_BLOCK), 0.1),
        "conv_w": nrm(ks[8], (DEPTH, CONV_WIDTH, XBC_WIDTH), CONV_WIDTH ** -0.5),
        "conv_b": nrm(ks[10], (DEPTH, XBC_WIDTH), 0.02),
        "dt_bias": dt_init + jnp.log(-jnp.expm1(-dt_init)),
        "A_log": jnp.log(jax.random.uniform(ks[11], (DEPTH, SSD_HEADS), f32, 1.0, 16.0)),
        "D_skip": 1.0 + nrm(ks[12], (DEPTH, SSD_HEADS), 0.1),
        "ssd_norm_w": 1.0 + nrm(ks[13], (DEPTH, SSD_WIDTH), 0.02),
        "w_out": nrm(ks[14], (DEPTH, D_MODEL, D_MODEL), D_MODEL ** -0.5),
        "final_norm_w": 1.0 + nrm(ks[15], (D_MODEL,), 0.02),
    }


def reference(x, norm_w, w_in, gate_b, sgu_norm_g, sgu_norm_b, sgu_w, sgu_b, conv_w, conv_b,
              dt_bias, A_log, D_skip, ssd_norm_w, w_out, final_norm_w):
    h = x
    Bsz, S, _ = x.shape
    for l in range(DEPTH):
        xn = rmsnorm(h, norm_w[l])
        proj = jnp.einsum("bsd,de->bse", xn, w_in[l])
        u_a, v_a, z_a, z_b, xbc, dt_raw, gate_logits = jnp.split(proj, IN_OFFSETS, axis=-1)
        y_a = sgu_mixer(u_a, v_a, z_a, sgu_norm_g[l], sgu_norm_b[l], sgu_w[l], sgu_b[l])
        y_b = ssd_mixer(xbc, z_b, dt_raw, conv_w[l], conv_b[l], dt_bias[l], A_log[l], D_skip[l], ssd_norm_w[l])
        gates = jax.nn.sigmoid(gate_logits + gate_b[l]).reshape(Bsz, S, N_BRANCHES, D_MODEL)
        merged = gates[:, :, 0, :] * y_a + gates[:, :, 1, :] * y_b
        h = h + jnp.einsum("bsd,de->bse", merged, w_out[l])
    return rmsnorm(h, final_norm_w)
```

```python
import functools

import jax
import jax.numpy as jnp
from jax import lax
from jax.experimental import pallas as pl
from jax.experimental.pallas import tpu as pltpu

CHUNK = 64
EPS = 1e-5
SGU_BLOCK = 128
SGU_GROUPS = 16
SSD_HEADDIM = 64
SSD_GROUPS = 4
SSD_STATE = 128
CONV_WIDTH = 4

LANES = 128
SUBLANES = 8

SEQ_TILE = 512
COL_BLOCK = 1024
SCAN_CHUNK = 128
ROW_CHUNK = 128
VMEM_LIMIT_BYTES = 58 * 1024 * 1024

F32 = jnp.float32
BF16 = jnp.bfloat16


def _silu(x):
    return x * jax.nn.sigmoid(x)


def _for_rows(n_rows, fn):
    def body(i, carry):
        fn(pl.multiple_of(i * ROW_CHUNK, ROW_CHUNK))
        return carry
    lax.fori_loop(0, n_rows // ROW_CHUNK, body, 0)


def _layer_body(
    x_ref, w_ref, normw_ref, gb0_ref, gb1_ref, lng_ref, lnb_ref, wm_ref, bexp_ref,
    convw_ref, convb_ref, dtb_ref, alog_ref, dskip_ref, ssdnw_ref, finw_ref,
    o_ref,
    lhs_s, p_s, v_s, vn_s, ya_s, merged_s, yssd_s, state_s, tail_s,
    *, apply_final_norm,
):
    t = pl.program_id(1)
    j = pl.program_id(2)
    T, D = lhs_s.shape
    HP = SSD_HEADDIM
    GW = dskip_ref.shape[-1]
    R = GW // HP
    N = SSD_STATE
    L = SCAN_CHUNK
    XBC_W = GW + 2 * N

    @pl.when(j == 0)
    def _():
        def blk(r0):
            xf = x_ref[pl.ds(r0, ROW_CHUNK), :]
            ms = jnp.mean(xf * xf, axis=-1, keepdims=True)
            lhs_s[pl.ds(r0, ROW_CHUNK), :] = (xf * lax.rsqrt(ms + EPS) * normw_ref[...]).astype(BF16)
        _for_rows(T, blk)

        @pl.when(t == 0)
        def _():
            state_s[...] = jnp.zeros_like(state_s)
            tail_s[...] = jnp.zeros_like(tail_s)

    p_s[...] = jnp.dot(lhs_s[...], w_ref[...], preferred_element_type=F32)

    @pl.when(j == 0)
    def _():
        v_s[:, 0:COL_BLOCK] = p_s[...]

    @pl.when(j == 1)
    def _():
        v_s[:, COL_BLOCK:2 * COL_BLOCK] = p_s[...]

        def blk(r0):
            v = v_s[pl.ds(r0, ROW_CHUNK), :]
            mu = jnp.mean(v, axis=-1, keepdims=True)
            d = v - mu
            var = jnp.mean(d * d, axis=-1, keepdims=True)
            y = d * lax.rsqrt(var + EPS) * lng_ref[...] + lnb_ref[...]
            for g in range(SGU_GROUPS):
                vn_s[g, pl.ds(r0, ROW_CHUNK), :] = y[:, g * LANES:(g + 1) * LANES].astype(BF16)
        _for_rows(T, blk)

    @pl.when((j == 2) | (j == 3) | (j == 5) | (j == 6))
    def _():
        cb = jnp.where(j < 4, j - 2, j - 3)
        half = cb % 2
        gpb = (COL_BLOCK // 2) // LANES
        row_c = lax.broadcasted_iota(jnp.int32, (SGU_BLOCK, SGU_BLOCK), 0) // CHUNK
        col_c = lax.broadcasted_iota(jnp.int32, (SGU_BLOCK, SGU_BLOCK), 1) // CHUNK
        keep = col_c <= row_c
        for gl in range(gpb):
            g = cb * gpb + gl
            wm = jnp.where(keep, wm_ref[g], 0.0).astype(BF16)
            bias = bexp_ref[g]
            for n in range(T // SGU_BLOCK):
                rows = slice(n * SGU_BLOCK, (n + 1) * SGU_BLOCK)
                mixed = jnp.dot(wm, vn_s[g, rows, :], preferred_element_type=F32) + bias
                u = p_s[rows, gl * LANES:(gl + 1) * LANES]
                z = p_s[rows, COL_BLOCK // 2 + gl * LANES:COL_BLOCK // 2 + (gl + 1) * LANES]
                ya_s[half, rows, gl * LANES:(gl + 1) * LANES] = u * mixed * _silu(z)

    @pl.when((j == 4) | (j == 7))
    def _():
        q = jnp.where(j == 4, 0, 1)
        hw = COL_BLOCK // 2

        def blk(r0):
            gate = jax.nn.sigmoid(p_s[pl.ds(r0, ROW_CHUNK), :] + gb0_ref[q])
            for k in range(2):
                merged_s[2 * q + k, pl.ds(r0, ROW_CHUNK), :] = (
                    gate[:, k * hw:(k + 1) * hw] * ya_s[k, pl.ds(r0, ROW_CHUNK), :])
        _for_rows(T, blk)

    @pl.when((j >= 8) & (j < 16) & (j % 2 == 0))
    def _():
        gi = (j - 8) // 2
        a_neg = -jnp.exp(alog_ref[gi])
        li = lax.broadcasted_iota(jnp.int32, (L, L), 0)
        si = lax.broadcasted_iota(jnp.int32, (L, L), 1)
        causal = li >= si
        lane = lax.broadcasted_iota(jnp.int32, (L, LANES), 1)
        row_in_chunk = lax.broadcasted_iota(jnp.int32, (L, LANES), 0)
        cw = convw_ref[gi]

        def chunk(c, carry):
            r0 = pl.multiple_of(c * L, L)
            prev0 = pl.multiple_of(jnp.maximum(r0 - SUBLANES, 0), SUBLANES)
            halo = jnp.where(c == 0, tail_s[gi], p_s[pl.ds(prev0, SUBLANES), 0:XBC_W])
            ext = jnp.concatenate([halo, p_s[pl.ds(r0, L), 0:XBC_W]], axis=0)
            acc = convb_ref[gi] + cw[CONV_WIDTH - 1:CONV_WIDTH, :] * ext
            for k in range(1, CONV_WIDTH):
                acc = acc + cw[CONV_WIDTH - 1 - k:CONV_WIDTH - k, :] * pltpu.roll(ext, k, 0)
            xbc = _silu(acc[SUBLANES:, :])
            xs = xbc[:, 0:GW]
            b_c = xbc[:, GW:GW + N].astype(BF16)
            c_c = xbc[:, GW + N:GW + 2 * N].astype(BF16)

            dt = jax.nn.softplus(p_s[pl.ds(r0, L), XBC_W:XBC_W + LANES] + dtb_ref[gi])
            acs = dt * a_neg
            k = 1
            while k < L:
                acs = acs + jnp.where(row_in_chunk >= k, pltpu.roll(acs, k, 0), 0.0)
                k *= 2
            acs_t = acs.T

            def per_head_cols(a):
                return [jnp.broadcast_to(a[:, r:r + 1], (L, LANES)) for r in range(R)]

            def expand(cols):
                per_vreg = LANES // HP
                parts = []
                for i in range(R // per_vreg):
                    piece = cols[i * per_vreg]
                    for m in range(1, per_vreg):
                        piece = jnp.where(lane >= m * HP, cols[i * per_vreg + m], piece)
                    parts.append(piece)
                return jnp.concatenate(parts, axis=1)

            acs_cols = per_head_cols(acs)
            acs_x = expand(acs_cols)
            dt_x = expand(per_head_cols(dt))
            xdt = xs * dt_x

            cbm = lax.dot_general(c_c, b_c, (((1,), (1,)), ((), ())), preferred_element_type=F32)
            ys = []
            for r in range(R):
                seg = acs_cols[r] - acs_t[r:r + 1, :]
                lmat = jnp.exp(jnp.where(causal, seg, -jnp.inf))
                m_r = (cbm * lmat).astype(BF16)
                ys.append(jnp.dot(m_r, xdt[:, r * HP:(r + 1) * HP].astype(BF16), preferred_element_type=F32))
            y_diag = jnp.concatenate(ys, axis=1)

            h = state_s[gi]
            y_off = jnp.dot(c_c, h.astype(BF16), preferred_element_type=F32) * jnp.exp(acs_x)
            last = acs_x[L - 1:L, :]
            w_st = jnp.exp(last - acs_x)
            st = lax.dot_general(b_c, (xdt * w_st).astype(BF16), (((0,), (0,)), ((), ())),
                                 preferred_element_type=F32)
            state_s[gi] = jnp.exp(last) * h + st
            yssd_s[pl.ds(r0, L), :] = y_diag + y_off + xs * dskip_ref[gi]
            return carry

        lax.fori_loop(0, T // L, chunk, 0)
        tail_s[gi] = p_s[T - SUBLANES:T, 0:XBC_W]

    @pl.when((j >= 8) & (j < 16) & (j % 2 == 1))
    def _():
        gi = (j - 9) // 2

        def blk(r0):
            y = yssd_s[pl.ds(r0, ROW_CHUNK), :]
            z = p_s[pl.ds(r0, ROW_CHUNK), 0:GW]
            hh = y * _silu(z)
            ms = jnp.mean(hh * hh, axis=-1, keepdims=True)
            hn = hh * lax.rsqrt(ms + EPS) * ssdnw_ref[gi]
            gate = jax.nn.sigmoid(p_s[pl.ds(r0, ROW_CHUNK), GW:2 * GW] + gb1_ref[gi])
            merged_s[gi, pl.ds(r0, ROW_CHUNK), :] = merged_s[gi, pl.ds(r0, ROW_CHUNK), :] + gate * hn
        _for_rows(T, blk)

        @pl.when(j == 15)
        def _():
            for k in range(D // GW):
                lhs_s[:, k * GW:(k + 1) * GW] = merged_s[k].astype(BF16)

    @pl.when(j == 16)
    def _():
        v_s[:, 0:COL_BLOCK] = x_ref[:, 0:COL_BLOCK] + p_s[...]

    @pl.when(j == 17)
    def _():
        v_s[:, COL_BLOCK:2 * COL_BLOCK] = x_ref[:, COL_BLOCK:2 * COL_BLOCK] + p_s[...]

        def blk(r0):
            hf = v_s[pl.ds(r0, ROW_CHUNK), :]
            if apply_final_norm:
                ms = jnp.mean(hf * hf, axis=-1, keepdims=True)
                hf = hf * lax.rsqrt(ms + EPS) * finw_ref[...]
            o_ref[pl.ds(r0, ROW_CHUNK), :] = hf
        _for_rows(T, blk)


def _pack_weights(w_in, w_out, d_model, n_heads):
    D = d_model
    G, N = SSD_GROUPS, SSD_STATE
    GW = D // G
    R = n_heads // G
    o_u, o_v, o_za, o_zb, o_xbc = 0, D, 2 * D, 3 * D, 4 * D
    o_dt = o_xbc + D + 2 * G * N
    o_g0 = o_dt + n_heads
    o_g1 = o_g0 + D
    half = COL_BLOCK // 2
    cols = lambda a, n: w_in[:, a:a + n]
    blocks = [cols(o_v, COL_BLOCK), cols(o_v + COL_BLOCK, COL_BLOCK)]
    for q in range(2):
        for cb in (2 * q, 2 * q + 1):
            blocks.append(jnp.concatenate([cols(o_u + cb * half, half), cols(o_za + cb * half, half)], axis=1))
        blocks.append(cols(o_g0 + q * COL_BLOCK, COL_BLOCK))
    for gi in range(G):
        pad_dt = jnp.zeros((D, LANES - R), w_in.dtype)
        pad = jnp.zeros((D, COL_BLOCK - (GW + 2 * N + LANES)), w_in.dtype)
        blocks.append(jnp.concatenate([
            cols(o_xbc + gi * GW, GW), cols(o_xbc + D + gi * N, N), cols(o_xbc + D + G * N + gi * N, N),
            cols(o_dt + gi * R, R), pad_dt, pad], axis=1))
        blocks.append(jnp.concatenate([cols(o_zb + gi * GW, GW), cols(o_g1 + gi * GW, GW)], axis=1))
    blocks += [w_out[:, 0:COL_BLOCK], w_out[:, COL_BLOCK:2 * COL_BLOCK]]
    return jnp.stack(blocks).astype(BF16)


def _layer(x, norm_w, w_in, gate_b, sgu_norm_g, sgu_norm_b, sgu_w, sgu_b, conv_w, conv_b,
           dt_bias, A_log, D_skip, ssd_norm_w, w_out, final_norm_w, *, apply_final_norm):
    B, S, D = x.shape
    H = A_log.shape[0]
    G, N, HP = SSD_GROUPS, SSD_STATE, SSD_HEADDIM
    GW = D // G
    R = H // G
    T = SEQ_TILE
    assert D == 2 * COL_BLOCK and GW == COL_BLOCK // 2 and H * HP == D and S % T == 0
    assert SGU_GROUPS * LANES == D and SGU_BLOCK == LANES and T % SCAN_CHUNK == 0 and T % SGU_BLOCK == 0
    assert GW + 2 * N + LANES <= COL_BLOCK and R <= LANES

    w_all = _pack_weights(w_in, w_out, D, H)
    n_steps = w_all.shape[0]
    assert n_steps == 18

    row = lambda a: a.reshape(1, -1).astype(F32)
    gb0 = gate_b[:D].reshape(2, 1, COL_BLOCK).astype(F32)
    gb1 = gate_b[D:].reshape(G, 1, GW).astype(F32)
    bexp = jnp.broadcast_to(sgu_b[:, :, None], (SGU_GROUPS, SGU_BLOCK, LANES)).astype(F32)
    xs_w = conv_w[:, :D].reshape(CONV_WIDTH, G, GW)
    b_w = conv_w[:, D:D + G * N].reshape(CONV_WIDTH, G, N)
    c_w = conv_w[:, D + G * N:].reshape(CONV_WIDTH, G, N)
    convw = jnp.transpose(jnp.concatenate([xs_w, b_w, c_w], axis=2), (1, 0, 2)).astype(F32)
    convb = jnp.concatenate([conv_b[:D].reshape(G, 1, GW), conv_b[D:D + G * N].reshape(G, 1, N),
                             conv_b[D + G * N:].reshape(G, 1, N)], axis=2).astype(F32)
    pad_heads = lambda a: jnp.pad(a.reshape(G, 1, R).astype(F32), ((0, 0), (0, 0), (0, LANES - R)))
    dtb = pad_heads(dt_bias)
    alog = pad_heads(A_log)
    dskip = jnp.repeat(D_skip.reshape(G, 1, R).astype(F32), HP, axis=2)
    ssdnw = ssd_norm_w.reshape(G, 1, GW).astype(F32)

    const2 = lambda shape: pl.BlockSpec(shape, lambda b, t, j: (0, 0))
    const3 = lambda shape: pl.BlockSpec(shape, lambda b, t, j: (0, 0, 0))
    in_specs = [
        pl.BlockSpec((None, T, D), lambda b, t, j: (b, t, 0)),
        pl.BlockSpec((None, D, COL_BLOCK), lambda b, t, j: (j, 0, 0)),
        const2((1, D)),
        const3((2, 1, COL_BLOCK)), const3((G, 1, GW)),
        const2((1, D)), const2((1, D)),
        const3((SGU_GROUPS, SGU_BLOCK, SGU_BLOCK)),
        const3((SGU_GROUPS, SGU_BLOCK, LANES)),
        const3((G, CONV_WIDTH, GW + 2 * N)), const3((G, 1, GW + 2 * N)),
        const3((G, 1, LANES)), const3((G, 1, LANES)),
        const3((G, 1, GW)), const3((G, 1, GW)),
        const2((1, D)),
    ]
    scratch = [
        pltpu.VMEM((T, D), BF16),
        pltpu.VMEM((T, COL_BLOCK), F32),
        pltpu.VMEM((T, D), F32),
        pltpu.VMEM((SGU_GROUPS, T, LANES), BF16),
        pltpu.VMEM((2, T, COL_BLOCK // 2), F32),
        pltpu.VMEM((D // GW, T, GW), F32),
        pltpu.VMEM((T, GW), F32),
        pltpu.VMEM((G, N, GW), F32),
        pltpu.VMEM((G, SUBLANES, GW + 2 * N), F32),
    ]
    return pl.pallas_call(
        functools.partial(_layer_body, apply_final_norm=apply_final_norm),
        out_shape=jax.ShapeDtypeStruct((B, S, D), x.dtype),
        grid=(B, S // T, n_steps),
        in_specs=in_specs,
        out_specs=pl.BlockSpec((None, T, D), lambda b, t, j: (b, t, 0)),
        scratch_shapes=scratch,
        compiler_params=pltpu.CompilerParams(
            dimension_semantics=("arbitrary", "arbitrary", "arbitrary"),
            vmem_limit_bytes=VMEM_LIMIT_BYTES),
        name="hybrid_sgu_ssd_layer",
    )(x, w_all, row(norm_w), gb0, gb1, row(sgu_norm_g), row(sgu_norm_b), sgu_w.astype(F32), bexp,
      convw, convb, dtb, alog, dskip, ssdnw, row(final_norm_w))


def kernel(x, norm_w, w_in, gate_b, sgu_norm_g, sgu_norm_b, sgu_w, sgu_b, conv_w, conv_b, dt_bias, A_log,
           D_skip, ssd_norm_w, w_out, final_norm_w):
    depth = w_in.shape[0]
    h = x
    for l in range(depth):
        h = _layer(h, norm_w[l], w_in[l], gate_b[l], sgu_norm_g[l], sgu_norm_b[l], sgu_w[l], sgu_b[l],
                   conv_w[l], conv_b[l], dt_bias[l], A_log[l], D_skip[l], ssd_norm_w[l], w_out[l],
                   final_norm_w, apply_final_norm=(l == depth - 1))
    return h
```
